```python
import math
import jax
import jax.numpy as jnp
from jax import lax
import numpy as np

D_MODEL = 1024
BATCH = 16
SEQ = 4096
DEPTH = 2

N_MIXERS = 2
N_GLA_LAYERS = (DEPTH + 1) // 2
N_HGRN_LAYERS = DEPTH // 2
CHUNK = 64
NORM_EPS = 1e-6

GLA_HEADS = 4
GLA_KEY = D_MODEL // 2
GLA_VAL = D_MODEL
GLA_DK = GLA_KEY // GLA_HEADS
GLA_DV = GLA_VAL // GLA_HEADS
GLA_GATE_RANK = 16
GLA_GATE_NORMALIZER = 16.0
GLA_IN = 2 * GLA_KEY + 2 * GLA_VAL + GLA_GATE_RANK

HG_EXPAND = 128
HG_HEADS = D_MODEL // HG_EXPAND
HG_DK = HG_EXPAND
HG_DV = D_MODEL // HG_HEADS
HG_KEY = HG_HEADS * HG_DK
HG_VAL = HG_HEADS * HG_DV
HG_IN = 2 * HG_KEY + 2 * HG_VAL

D_FF = int(math.ceil(8 * D_MODEL / 3 / 256)) * 256

kernel_name = "gla_hgrn2_interleaved_hybrid"


def rms_norm(x, gain):
    xf = x.astype(jnp.float32)
    inv = lax.rsqrt(jnp.mean(xf * xf, axis=-1, keepdims=True) + NORM_EPS)
    return (xf * inv).astype(x.dtype) * gain


def chunked_gated_linear_attention(q, k, v, g_log):
    out_dtype = v.dtype
    B, S, H, DK = q.shape
    DV = v.shape[-1]
    N = S // CHUNK

    def to_chunks(t):
        return t.astype(jnp.float32).reshape(B, N, CHUNK, H, t.shape[-1]).transpose(1, 0, 3, 2, 4)

    qc, kc, vc, gc = to_chunks(q), to_chunks(k), to_chunks(v), to_chunks(g_log)
    causal = jnp.tril(jnp.ones((CHUNK, CHUNK), dtype=bool))[:, :, None]

    def step(state, inp):
        q_c, k_c, v_c, g_c = inp
        b = jnp.cumsum(g_c, axis=2)
        o_inter = jnp.einsum('bhcd,bhde->bhce', q_c * jnp.exp(b), state)
        diff = b[:, :, :, None, :] - b[:, :, None, :, :]
        decay = jnp.exp(jnp.where(causal, diff, -jnp.inf))
        scores = jnp.einsum('bhid,bhjd,bhijd->bhij', q_c, k_c, decay)
        o = o_inter + jnp.einsum('bhij,bhje->bhie', scores, v_c)
        b_last = b[:, :, -1:, :]
        k_dec = k_c * jnp.exp(b_last - b)
        state = jnp.exp(b_last[:, :, 0, :])[..., None] * state + jnp.einsum('bhcd,bhce->bhde', k_dec, v_c)
        return state, o

    state0 = jnp.zeros((B, H, DK, DV), jnp.float32)
    _, o = lax.scan(step, state0, (qc, kc, vc, gc))
    return o.transpose(1, 0, 3, 2, 4).reshape(B, S, H, DV).astype(out_dtype)


def gla_mixer(h, w_in, w_gate_up, b_gate, head_norm, w_out):
    B, S, _ = h.shape
    proj = h @ w_in
    q, k, v, r, gd = jnp.split(
        proj, [GLA_KEY, 2 * GLA_KEY, 2 * GLA_KEY + GLA_VAL, 2 * GLA_KEY + 2 * GLA_VAL], axis=-1)
    g_log = jax.nn.log_sigmoid((gd @ w_gate_up + b_gate).astype(jnp.float32)) / GLA_GATE_NORMALIZER
    q = q.reshape(B, S, GLA_HEADS, GLA_DK) * (GLA_DK ** -0.5)
    k = k.reshape(B, S, GLA_HEADS, GLA_DK)
    v = v.reshape(B, S, GLA_HEADS, GLA_DV)
    g_log = g_log.reshape(B, S, GLA_HEADS, GLA_DK)
    o = chunked_gated_linear_attention(q, k, v, g_log)
    o = rms_norm(o, head_norm) * jax.nn.silu(r.reshape(B, S, GLA_HEADS, GLA_DV))
    return o.reshape(B, S, GLA_VAL) @ w_out


def hgrn2_mixer(h, w_in, lb, out_norm, w_out):
    B, S, _ = h.shape
    proj = h @ w_in
    q, f, i, g = jnp.split(proj, [HG_KEY, 2 * HG_KEY, 2 * HG_KEY + HG_VAL], axis=-1)
    q = jax.nn.silu(q)
    f = f.astype(jnp.float32)
    log_forget = jnp.logaddexp(jnp.log(lb), jnp.log1p(-lb) + jax.nn.log_sigmoid(f))
    k = (1.0 - lb) * jax.nn.sigmoid(-f)
    q = q.reshape(B, S, HG_HEADS, HG_DK) * (HG_DK ** -0.5)
    k = k.reshape(B, S, HG_HEADS, HG_DK)
    log_forget = log_forget.reshape(B, S, HG_HEADS, HG_DK)
    v = i.reshape(B, S, HG_HEADS, HG_DV)
    o = chunked_gated_linear_attention(q, k, v, log_forget).reshape(B, S, HG_VAL)
    o = rms_norm(o, out_norm) * jax.nn.silu(g)
    return o @ w_out


def swiglu_ffn(h, w_in, w_out):
    gate, up = jnp.split(h @ w_in, [D_FF], axis=-1)
    return (jax.nn.silu(gate) * up) @ w_out


def setup_inputs(seed: int = 0) -> dict:
    key = jax.random.key(seed)
    ks = jax.random.split(key, 20)
    f32 = jnp.float32

    def w(k, shape, fan_in):
        return jax.random.normal(k, shape, f32) * (fan_in ** -0.5)

    def gain(k, shape):
        return 1.0 + 0.02 * jax.random.normal(k, shape, f32)

    return {
        "x": jax.random.normal(ks[0], (BATCH, SEQ, D_MODEL), f32),
        "mixer_norm": gain(ks[1], (DEPTH, D_MODEL)),
        "ffn_norm": gain(ks[2], (DEPTH, D_MODEL)),
        "gla_w_in": w(ks[3], (N_GLA_LAYERS, D_MODEL, GLA_IN), D_MODEL),
        "gla_w_gate_up": w(ks[4], (N_GLA_LAYERS, GLA_GATE_RANK, GLA_KEY), GLA_GATE_RANK),
        "gla_b_gate": 0.1 * jax.random.normal(ks[5], (N_GLA_LAYERS, GLA_KEY), f32),
        "gla_head_norm": gain(ks[6], (N_GLA_LAYERS, GLA_DV)),
        "gla_w_out": w(ks[7], (N_GLA_LAYERS, GLA_VAL, D_MODEL), GLA_VAL),
        "hgrn_w_in": w(ks[8], (N_HGRN_LAYERS, D_MODEL, HG_IN), D_MODEL),
        "hgrn_lower_bounds": 0.5 * jax.random.normal(ks[9], (DEPTH, HG_KEY), f32),
        "hgrn_out_norm": gain(ks[10], (N_HGRN_LAYERS, HG_VAL)),
        "hgrn_w_out": w(ks[11], (N_HGRN_LAYERS, HG_VAL, D_MODEL), HG_VAL),
        "ffn_w_in": w(ks[12], (DEPTH, D_MODEL, 2 * D_FF), D_MODEL),
        "ffn_w_out": w(ks[13], (DEPTH, D_FF, D_MODEL), D_FF),
        "final_norm": gain(ks[14], (D_MODEL,)),
    }


def reference(x, mixer_norm, ffn_norm, gla_w_in, gla_w_gate_up, gla_b_gate, gla_head_norm,
              gla_w_out, hgrn_w_in, hgrn_lower_bounds, hgrn_out_norm, hgrn_w_out,
              ffn_w_in, ffn_w_out, final_norm):
    lb_soft = jax.nn.softmax(hgrn_lower_bounds.astype(jnp.float32), axis=0)
    lb_table = jnp.cumsum(lb_soft, axis=0) - lb_soft[0]

    for layer in range(DEPTH):
        h = rms_norm(x, mixer_norm[layer])
        j = layer // N_MIXERS
        if layer % N_MIXERS == 0:
            mixed = gla_mixer(h, gla_w_in[j], gla_w_gate_up[j], gla_b_gate[j],
                              gla_head_norm[j], gla_w_out[j])
        else:
            mixed = hgrn2_mixer(h, hgrn_w_in[j], lb_table[layer], hgrn_out_norm[j], hgrn_w_out[j])
        x = x + mixed
        x = x + swiglu_ffn(rms_norm(x, ffn_norm[layer]), ffn_w_in[layer], ffn_w_out[layer])
    return rms_norm(x, final_norm)
```

```python
import functools
import math

import jax
import jax.numpy as jnp
from jax import lax
from jax.experimental import pallas as pl
from jax.experimental.pallas import tpu as pltpu

D_MODEL = 1024
DEPTH = 2
N_MIXERS = 2
NORM_EPS = 1e-6

GLA_HEADS = 4
GLA_KEY = D_MODEL // 2
GLA_VAL = D_MODEL
GLA_DK = GLA_KEY // GLA_HEADS
GLA_DV = GLA_VAL // GLA_HEADS
GLA_GATE_RANK = 16
GLA_GATE_NORMALIZER = 16.0

HG_EXPAND = 128
HG_HEADS = D_MODEL // HG_EXPAND
HG_DK = HG_EXPAND
HG_DV = D_MODEL // HG_HEADS
HG_KEY = HG_HEADS * HG_DK
HG_VAL = HG_HEADS * HG_DV

D_FF = int(math.ceil(8 * D_MODEL / 3 / 256)) * 256

LANES = 128
SUBLANES = 8
VMEM_LIMIT_BYTES = 56 * 1024 * 1024

SCAN_CHUNK = 64
SCAN_LEVELS = tuple(SCAN_CHUNK >> (i + 1) for i in range(int(math.log2(SCAN_CHUNK))))

ROW_TILE = 512
SCAN_TILE = 1024
FF_TILE = 1408

ACT_DTYPE = jnp.float32
MXU_DTYPE = jnp.bfloat16


def _dot(a, b):
    return jnp.dot(a.astype(MXU_DTYPE), b.astype(MXU_DTYPE), preferred_element_type=jnp.float32)


def _dot_nt(a, b):
    return lax.dot_general(a.astype(MXU_DTYPE), b.astype(MXU_DTYPE), (((1,), (1,)), ((), ())),
                           preferred_element_type=jnp.float32)


def _rms_norm(xf, gain):
    inv = lax.rsqrt(jnp.mean(xf * xf, axis=-1, keepdims=True) + NORM_EPS)
    return xf * inv * gain


def _log_sigmoid(z):
    return jnp.minimum(z, 0.0) - jnp.log1p(jnp.exp(-jnp.abs(z)))


def _silu(z):
    return z * (1.0 / (1.0 + jnp.exp(-z)))


def _resident(shape):
    return pl.BlockSpec(shape, lambda *_: (0,) * len(shape), pipeline_mode=pl.Buffered(1))


def _gla_in_kernel(x_ref, gain_ref, wq_ref, wk_ref, wv_ref, wr_ref, wgd_ref, wgu_ref, bg_ref,
                   q_ref, k_ref, v_ref, r_ref, g_ref):
    h = _rms_norm(x_ref[...], gain_ref[...]).astype(MXU_DTYPE)
    q_ref[...] = (_dot(h, wq_ref[...]) * (GLA_DK ** -0.5)).astype(q_ref.dtype)
    k_ref[...] = _dot(h, wk_ref[...]).astype(k_ref.dtype)
    v_ref[...] = _dot(h, wv_ref[...]).astype(v_ref.dtype)
    r_ref[...] = _dot(h, wr_ref[...]).astype(r_ref.dtype)
    gd = _dot(h, wgd_ref[...])
    z = _dot(gd, wgu_ref[...]) + bg_ref[...]
    g_ref[...] = _log_sigmoid(z) / GLA_GATE_NORMALIZER


def _gla_in_proj(x2, gain, w_in, w_gate_up, b_gate):
    m = x2.shape[0]
    kk, vv = GLA_KEY, GLA_VAL
    wq = w_in[:, :kk].astype(MXU_DTYPE)
    wk = w_in[:, kk:2 * kk].astype(MXU_DTYPE)
    wv = w_in[:, 2 * kk:2 * kk + vv].astype(MXU_DTYPE)
    wr = w_in[:, 2 * kk + vv:2 * kk + 2 * vv].astype(MXU_DTYPE)
    pad = LANES - GLA_GATE_RANK
    wgd = jnp.pad(w_in[:, 2 * kk + 2 * vv:], ((0, 0), (0, pad))).astype(MXU_DTYPE)
    wgu = jnp.pad(w_gate_up, ((0, pad), (0, 0))).astype(MXU_DTYPE)
    row = lambda n: pl.BlockSpec((ROW_TILE, n), lambda i: (i, 0))
    return pl.pallas_call(
        _gla_in_kernel,
        grid=(m // ROW_TILE,),
        in_specs=[row(D_MODEL), _resident((1, D_MODEL)),
                  _resident((D_MODEL, kk)), _resident((D_MODEL, kk)),
                  _resident((D_MODEL, vv)), _resident((D_MODEL, vv)),
                  _resident((D_MODEL, LANES)), _resident((LANES, kk)), _resident((1, kk))],
        out_specs=[row(kk), row(kk), row(vv), row(vv), row(kk)],
        out_shape=[jax.ShapeDtypeStruct((m, kk), ACT_DTYPE), jax.ShapeDtypeStruct((m, kk), ACT_DTYPE),
                   jax.ShapeDtypeStruct((m, vv), ACT_DTYPE), jax.ShapeDtypeStruct((m, vv), ACT_DTYPE),
                   jax.ShapeDtypeStruct((m, kk), jnp.float32)],
        compiler_params=pltpu.CompilerParams(dimension_semantics=("arbitrary",),
                                             vmem_limit_bytes=VMEM_LIMIT_BYTES),
        name="gla_in_proj",
    )(x2, gain.reshape(1, -1), wq, wk, wv, wr, wgd, wgu, b_gate.reshape(1, -1))


def _hgrn_in_kernel(layer, x_ref, gain_ref, lbp_ref, wq_ref, wf_ref, wi_ref, wg_ref,
                    q_ref, k_ref, lf_ref, v_ref, gate_ref):
    h = _rms_norm(x_ref[...], gain_ref[...]).astype(MXU_DTYPE)
    lbp = lbp_ref[...]
    e = jnp.exp(lbp - jnp.max(lbp, axis=0, keepdims=True))
    soft = e / jnp.sum(e, axis=0, keepdims=True)
    lb = jnp.sum(soft[:layer + 1], axis=0, keepdims=True) - soft[0:1]
    q = _dot(h, wq_ref[...])
    q_ref[...] = (_silu(q) * (HG_DK ** -0.5)).astype(q_ref.dtype)
    f = _dot(h, wf_ref[...])
    a = jnp.log(lb)
    c = jnp.log1p(-lb) + _log_sigmoid(f)
    lf_ref[...] = jnp.maximum(a, c) + jnp.log1p(jnp.exp(-jnp.abs(a - c)))
    k_ref[...] = ((1.0 - lb) * (1.0 / (1.0 + jnp.exp(f)))).astype(k_ref.dtype)
    v_ref[...] = _dot(h, wi_ref[...]).astype(v_ref.dtype)
    gate_ref[...] = _dot(h, wg_ref[...]).astype(gate_ref.dtype)


def _hgrn_in_proj(x2, gain, lower_bounds, layer, w_in):
    m = x2.shape[0]
    n = HG_KEY
    ws = [w_in[:, i * n:(i + 1) * n].astype(MXU_DTYPE) for i in range(4)]
    row = lambda: pl.BlockSpec((ROW_TILE, n), lambda i: (i, 0))
    return pl.pallas_call(
        functools.partial(_hgrn_in_kernel, layer),
        grid=(m // ROW_TILE,),
        in_specs=[pl.BlockSpec((ROW_TILE, D_MODEL), lambda i: (i, 0)), _resident((1, D_MODEL)),
                  _resident((DEPTH, n))] + [_resident((D_MODEL, n))] * 4,
        out_specs=[row(), row(), row(), row(), row()],
        out_shape=[jax.ShapeDtypeStruct((m, n), ACT_DTYPE), jax.ShapeDtypeStruct((m, n), ACT_DTYPE),
                   jax.ShapeDtypeStruct((m, n), jnp.float32), jax.ShapeDtypeStruct((m, n), ACT_DTYPE),
                   jax.ShapeDtypeStruct((m, n), ACT_DTYPE)],
        compiler_params=pltpu.CompilerParams(dimension_semantics=("arbitrary",),
                                             vmem_limit_bytes=VMEM_LIMIT_BYTES),
        name="hgrn_in_proj",
    )(x2, gain.reshape(1, -1), lower_bounds, *ws)


def _level_table():
    i = jnp.arange(SCAN_CHUNK)[:, None]
    j = jnp.arange(SCAN_CHUNK)[None, :]
    table = jnp.full((SCAN_CHUNK, SCAN_CHUNK), -1, jnp.int32)
    for idx, s in enumerate(SCAN_LEVELS):
        hit = (i // (2 * s) == j // (2 * s)) & (i % (2 * s) >= s) & (j % (2 * s) < s)
        table = jnp.where(hit, idx, table)
    return jnp.where(i == j, len(SCAN_LEVELS), table)


def _cumsum_rows(g, row):
    b = g
    shift = 1
    while shift < SCAN_CHUNK:
        b = b + jnp.where(row >= shift, pltpu.roll(b, shift, axis=0), 0.0)
        shift *= 2
    return b


def _level_reference(b, b_ref, s, row):
    c = SCAN_CHUNK
    if s == 1:
        return jnp.where(row % 2 == 1, pltpu.roll(b, 1, axis=0), b)
    if s == 2:
        r4 = row % 4
        return jnp.where(r4 == 0, pltpu.roll(b, c - 1, axis=0),
                         jnp.where(r4 == 1, b,
                                   jnp.where(r4 == 2, pltpu.roll(b, 1, axis=0), pltpu.roll(b, 2, axis=0))))
    parts = [jnp.broadcast_to(b_ref[m * 2 * s + s - 1:m * 2 * s + s, :], (2 * s, b.shape[1]))
             for m in range(c // (2 * s))]
    return parts[0] if len(parts) == 1 else jnp.concatenate(parts, axis=0)


def _scan_kernel(q_ref, k_ref, g_ref, v_ref, lv_ref, o_ref, state_ref, b_ref):
    c = SCAN_CHUNK
    dk = q_ref.shape[-1]

    @pl.when(pl.program_id(2) == 0)
    def _():
        state_ref[...] = jnp.zeros_like(state_ref)

    row = lax.broadcasted_iota(jnp.int32, (c, dk), 0)
    level_of = lv_ref[...]

    def chunk(ci, carry):
        rows = pl.ds(pl.multiple_of(ci * c, c), c)
        q = q_ref[0, rows, :].astype(jnp.float32)
        k = k_ref[0, rows, :].astype(jnp.float32)
        v = v_ref[0, rows, :].astype(jnp.float32)
        b = _cumsum_rows(g_ref[0, rows, :], row)
        b_ref[...] = b
        state_t = state_ref[...]

        scores = jnp.where(level_of == len(SCAN_LEVELS), _dot_nt(q, k), 0.0)
        for idx, s in enumerate(SCAN_LEVELS):
            mid = _level_reference(b, b_ref, s, row)
            lower = (row % (2 * s)) >= s
            e = jnp.exp(jnp.where(lower, b - mid, mid - b))
            scores = jnp.where(level_of == idx, _dot_nt(q * e, k * e), scores)

        o = _dot_nt(q * jnp.exp(b), state_t) + _dot(scores, v)
        o_ref[0, rows, :] = o.astype(o_ref.dtype)

        b_last = b_ref[c - 1:c, :]
        k_dec = k * jnp.exp(b_last - b)
        state_ref[...] = state_t * jnp.exp(b_last) + _dot(v.T, k_dec)
        return carry

    lax.fori_loop(0, q_ref.shape[1] // c, chunk, 0)


def _chunk_scan(q, k, g, v, heads):
    bsz, seq, _ = q.shape
    dk = q.shape[-1] // heads
    dv = v.shape[-1] // heads
    spec = lambda d: pl.BlockSpec((1, SCAN_TILE, d), lambda b, h, t: (b, t, h))
    return pl.pallas_call(
        _scan_kernel,
        grid=(bsz, heads, seq // SCAN_TILE),
        in_specs=[spec(dk), spec(dk), spec(dk), spec(dv),
                  pl.BlockSpec((SCAN_CHUNK, SCAN_CHUNK), lambda b, h, t: (0, 0))],
        out_specs=spec(dv),
        out_shape=jax.ShapeDtypeStruct(v.shape, jnp.float32),
        scratch_shapes=[pltpu.VMEM((dv, dk), jnp.float32), pltpu.VMEM((SCAN_CHUNK, dk), jnp.float32)],
        compiler_params=pltpu.CompilerParams(dimension_semantics=("arbitrary", "arbitrary", "arbitrary"),
                                             vmem_limit_bytes=VMEM_LIMIT_BYTES),
        name="chunk_scan",
    )(q, k, g, v, _level_table())


def _post_ffn_kernel(norm_width, apply_final, o_ref, gate_ref, x_ref, onorm_ref, wo_ref, fnorm_ref,
                     wg_ref, wu_ref, wd_ref, final_ref, y_ref):
    width = o_ref.shape[-1]
    mixed = None
    for lo in range(0, width, norm_width):
        seg = o_ref[:, lo:lo + norm_width]
        normed = _rms_norm(seg, onorm_ref[:, lo:lo + norm_width])
        gated = normed * _silu(gate_ref[:, lo:lo + norm_width].astype(jnp.float32))
        part = _dot(gated, wo_ref[lo:lo + norm_width, :])
        mixed = part if mixed is None else mixed + part
    x1 = x_ref[...] + mixed
    h = _rms_norm(x1, fnorm_ref[...]).astype(MXU_DTYPE)
    y = x1
    for lo in range(0, D_FF, FF_TILE):
        act = _silu(_dot(h, wg_ref[:, lo:lo + FF_TILE])) * _dot(h, wu_ref[:, lo:lo + FF_TILE])
        y = y + _dot(act, wd_ref[lo:lo + FF_TILE, :])
    if apply_final:
        y = _rms_norm(y, final_ref[...])
    y_ref[...] = y


def _post_ffn(o2, gate2, x2, out_norm, norm_width, w_out, ffn_norm, ffn_w_in, ffn_w_out, final_norm,
              apply_final):
    m = x2.shape[0]
    row = lambda: pl.BlockSpec((ROW_TILE, D_MODEL), lambda i: (i, 0))
    wg = ffn_w_in[:, :D_FF].astype(MXU_DTYPE)
    wu = ffn_w_in[:, D_FF:].astype(MXU_DTYPE)
    return pl.pallas_call(
        functools.partial(_post_ffn_kernel, norm_width, apply_final),
        grid=(m // ROW_TILE,),
        in_specs=[row(), row(), row(), _resident((1, D_MODEL)), _resident((D_MODEL, D_MODEL)),
                  _resident((1, D_MODEL)), _resident((D_MODEL, D_FF)), _resident((D_MODEL, D_FF)),
                  _resident((D_FF, D_MODEL)), _resident((1, D_MODEL))],
        out_specs=row(),
        out_shape=jax.ShapeDtypeStruct((m, D_MODEL), jnp.float32),
        compiler_params=pltpu.CompilerParams(dimension_semantics=("arbitrary",),
                                             vmem_limit_bytes=VMEM_LIMIT_BYTES),
        name="post_ffn",
    )(o2, gate2, x2, out_norm.reshape(1, -1), w_out.astype(MXU_DTYPE), ffn_norm.reshape(1, -1),
      wg, wu, ffn_w_out.astype(MXU_DTYPE), final_norm.reshape(1, -1))


def kernel(x, mixer_norm, ffn_norm, gla_w_in, gla_w_gate_up, gla_b_gate, gla_head_norm, gla_w_out,
           hgrn_w_in, hgrn_lower_bounds, hgrn_out_norm, hgrn_w_out, ffn_w_in, ffn_w_out, final_norm):
    bsz, seq, d = x.shape
    m = bsz * seq
    x2 = x.reshape(m, d)
    for layer in range(DEPTH):
        j = layer // N_MIXERS
        last = layer == DEPTH - 1
        if layer % N_MIXERS == 0:
            q, k, v, r, g = _gla_in_proj(x2, mixer_norm[layer], gla_w_in[j], gla_w_gate_up[j], gla_b_gate[j])
            shp = lambda t: t.reshape(bsz, seq, t.shape[-1])
            o = _chunk_scan(shp(q), shp(k), shp(g), shp(v), GLA_HEADS)
            x2 = _post_ffn(o.reshape(m, GLA_VAL), r, x2, jnp.tile(gla_head_norm[j], GLA_HEADS), GLA_DV,
                           gla_w_out[j], ffn_norm[layer], ffn_w_in[layer], ffn_w_out[layer], final_norm, last)
        else:
            q, k, lf, v, gate = _hgrn_in_proj(x2, mixer_norm[layer], hgrn_lower_bounds, layer, hgrn_w_in[j])
            shp = lambda t: t.reshape(bsz, seq, t.shape[-1])
            o = _chunk_scan(shp(q), shp(k), shp(lf), shp(v), HG_HEADS)
            x2 = _post_ffn(o.reshape(m, HG_VAL), gate, x2, hgrn_out_norm[j], HG_VAL,
                           hgrn_w_out[j], ffn_norm[layer], ffn_w_in[layer], ffn_w_out[layer], final_norm, last)
    return x2.reshape(bsz, seq, d)
```

```python
import functools
import math

import jax
import jax.numpy as jnp
from jax import lax
from jax.experimental import pallas as pl
from jax.experimental.pallas import tpu as pltpu

D_MODEL = 1024
DEPTH = 2
N_MIXERS = 2
NORM_EPS = 1e-6

GLA_HEADS = 4
GLA_KEY = D_MODEL // 2
GLA_VAL = D_MODEL
GLA_DK = GLA_KEY // GLA_HEADS
GLA_DV = GLA_VAL // GLA_HEADS
GLA_GATE_RANK = 16
GLA_GATE_NORMALIZER = 16.0

HG_EXPAND = 128
HG_HEADS = D_MODEL // HG_EXPAND
HG_DK = HG_EXPAND
HG_DV = D_MODEL // HG_HEADS
HG_KEY = HG_HEADS * HG_DK
HG_VAL = HG_HEADS * HG_DV

D_FF = int(math.ceil(8 * D_MODEL / 3 / 256)) * 256

LANES = 128
SUBLANES = 8
VMEM_LIMIT_BYTES = 56 * 1024 * 1024

SCAN_CHUNK = 64
SCAN_LEVELS = tuple(SCAN_CHUNK >> (i + 1) for i in range(int(math.log2(SCAN_CHUNK))))

ROW_TILE = 512
SCAN_TILE = 1024
FF_TILE = 1408
SCAN_UNROLL = 8

ACT_DTYPE = jnp.float32
MXU_DTYPE = jnp.bfloat16
LOG2_E = 1.4426950408889634


def _dot(a, b):
    return jnp.dot(a.astype(MXU_DTYPE), b.astype(MXU_DTYPE), preferred_element_type=jnp.float32)


def _dot_nt(a, b):
    return lax.dot_general(a.astype(MXU_DTYPE), b.astype(MXU_DTYPE), (((1,), (1,)), ((), ())),
                           preferred_element_type=jnp.float32)


def _rms_norm(xf, gain):
    inv = lax.rsqrt(jnp.mean(xf * xf, axis=-1, keepdims=True) + NORM_EPS)
    return xf * inv * gain


def _log_sigmoid(z):
    return jnp.minimum(z, 0.0) - jnp.log1p(jnp.exp(-jnp.abs(z)))


def _silu(z):
    return z * (1.0 / (1.0 + jnp.exp(-z)))


def _resident(shape):
    return pl.BlockSpec(shape, lambda *_: (0,) * len(shape), pipeline_mode=pl.Buffered(1))


def _gla_in_kernel(x_ref, gain_ref, wq_ref, wk_ref, wv_ref, wr_ref, wgd_ref, wgu_ref, bg_ref,
                   q_ref, k_ref, v_ref, r_ref, g_ref):
    h = _rms_norm(x_ref[...], gain_ref[...]).astype(MXU_DTYPE)
    q_ref[...] = (_dot(h, wq_ref[...]) * (GLA_DK ** -0.5)).astype(q_ref.dtype)
    k_ref[...] = _dot(h, wk_ref[...]).astype(k_ref.dtype)
    v_ref[...] = _dot(h, wv_ref[...]).astype(v_ref.dtype)
    r_ref[...] = _dot(h, wr_ref[...]).astype(r_ref.dtype)
    gd = _dot(h, wgd_ref[...])
    z = _dot(gd, wgu_ref[...]) + bg_ref[...]
    g_ref[...] = _log_sigmoid(z) / GLA_GATE_NORMALIZER


def _gla_in_proj(x2, gain, w_in, w_gate_up, b_gate):
    m = x2.shape[0]
    kk, vv = GLA_KEY, GLA_VAL
    wq = w_in[:, :kk].astype(MXU_DTYPE)
    wk = w_in[:, kk:2 * kk].astype(MXU_DTYPE)
    wv = w_in[:, 2 * kk:2 * kk + vv].astype(MXU_DTYPE)
    wr = w_in[:, 2 * kk + vv:2 * kk + 2 * vv].astype(MXU_DTYPE)
    pad = LANES - GLA_GATE_RANK
    wgd = jnp.pad(w_in[:, 2 * kk + 2 * vv:], ((0, 0), (0, pad))).astype(MXU_DTYPE)
    wgu = jnp.pad(w_gate_up, ((0, pad), (0, 0))).astype(MXU_DTYPE)
    row = lambda n: pl.BlockSpec((ROW_TILE, n), lambda i: (i, 0))
    return pl.pallas_call(
        _gla_in_kernel,
        grid=(m // ROW_TILE,),
        in_specs=[row(D_MODEL), _resident((1, D_MODEL)),
                  _resident((D_MODEL, kk)), _resident((D_MODEL, kk)),
                  _resident((D_MODEL, vv)), _resident((D_MODEL, vv)),
                  _resident((D_MODEL, LANES)), _resident((LANES, kk)), _resident((1, kk))],
        out_specs=[row(kk), row(kk), row(vv), row(vv), row(kk)],
        out_shape=[jax.ShapeDtypeStruct((m, kk), ACT_DTYPE), jax.ShapeDtypeStruct((m, kk), ACT_DTYPE),
                   jax.ShapeDtypeStruct((m, vv), ACT_DTYPE), jax.ShapeDtypeStruct((m, vv), ACT_DTYPE),
                   jax.ShapeDtypeStruct((m, kk), jnp.float32)],
        compiler_params=pltpu.CompilerParams(dimension_semantics=("arbitrary",),
                                             vmem_limit_bytes=VMEM_LIMIT_BYTES),
        name="gla_in_proj",
    )(x2, gain.reshape(1, -1), wq, wk, wv, wr, wgd, wgu, b_gate.reshape(1, -1))


def _hgrn_in_kernel(layer, x_ref, gain_ref, lbp_ref, wq_ref, wf_ref, wi_ref, wg_ref,
                    q_ref, k_ref, lf_ref, v_ref, gate_ref):
    h = _rms_norm(x_ref[...], gain_ref[...]).astype(MXU_DTYPE)
    lbp = lbp_ref[...]
    e = jnp.exp(lbp - jnp.max(lbp, axis=0, keepdims=True))
    soft = e / jnp.sum(e, axis=0, keepdims=True)
    lb = jnp.sum(soft[:layer + 1], axis=0, keepdims=True) - soft[0:1]
    q = _dot(h, wq_ref[...])
    q_ref[...] = (_silu(q) * (HG_DK ** -0.5)).astype(q_ref.dtype)
    f = _dot(h, wf_ref[...])
    a = jnp.log(lb)
    c = jnp.log1p(-lb) + _log_sigmoid(f)
    lf_ref[...] = jnp.maximum(a, c) + jnp.log1p(jnp.exp(-jnp.abs(a - c)))
    k_ref[...] = ((1.0 - lb) * (1.0 / (1.0 + jnp.exp(f)))).astype(k_ref.dtype)
    v_ref[...] = _dot(h, wi_ref[...]).astype(v_ref.dtype)
    gate_ref[...] = _dot(h, wg_ref[...]).astype(gate_ref.dtype)


def _hgrn_in_proj(x2, gain, lower_bounds, layer, w_in):
    m = x2.shape[0]
    n = HG_KEY
    ws = [w_in[:, i * n:(i + 1) * n].astype(MXU_DTYPE) for i in range(4)]
    row = lambda: pl.BlockSpec((ROW_TILE, n), lambda i: (i, 0))
    return pl.pallas_call(
        functools.partial(_hgrn_in_kernel, layer),
        grid=(m // ROW_TILE,),
        in_specs=[pl.BlockSpec((ROW_TILE, D_MODEL), lambda i: (i, 0)), _resident((1, D_MODEL)),
                  _resident((DEPTH, n))] + [_resident((D_MODEL, n))] * 4,
        out_specs=[row(), row(), row(), row(), row()],
        out_shape=[jax.ShapeDtypeStruct((m, n), ACT_DTYPE), jax.ShapeDtypeStruct((m, n), ACT_DTYPE),
                   jax.ShapeDtypeStruct((m, n), jnp.float32), jax.ShapeDtypeStruct((m, n), ACT_DTYPE),
                   jax.ShapeDtypeStruct((m, n), ACT_DTYPE)],
        compiler_params=pltpu.CompilerParams(dimension_semantics=("arbitrary",),
                                             vmem_limit_bytes=VMEM_LIMIT_BYTES),
        name="hgrn_in_proj",
    )(x2, gain.reshape(1, -1), lower_bounds, *ws)


def _level_constants(dk):
    c = SCAN_CHUNK
    i = jnp.arange(c)[:, None]
    j = jnp.arange(c)[None, :]
    masks = [(i // (2 * s) == j // (2 * s)) & (i % (2 * s) >= s) & (j % (2 * s) < s) for s in SCAN_LEVELS]
    masks.append(i == j)
    signs = [jnp.where(jnp.arange(c) % (2 * s) >= s, 1.0, -1.0) for s in SCAN_LEVELS]
    signs = jnp.broadcast_to(jnp.stack(signs)[:, :, None], (len(SCAN_LEVELS), c, dk))
    return jnp.stack(masks).astype(MXU_DTYPE), signs.astype(jnp.float32)


def _cumsum_rows(g, row):
    b = g
    shift = 1
    while shift < SCAN_CHUNK:
        b = b + jnp.where(row >= shift, pltpu.roll(b, shift, axis=0), 0.0)
        shift *= 2
    return b


def _level_reference(b, b_ref, ci, s, row):
    c = SCAN_CHUNK
    if s == 1:
        return jnp.where(row % 2 == 1, pltpu.roll(b, 1, axis=0), b)
    if s == 2:
        r4 = row % 4
        return jnp.where(r4 == 0, pltpu.roll(b, c - 1, axis=0),
                         jnp.where(r4 == 1, b,
                                   jnp.where(r4 == 2, pltpu.roll(b, 1, axis=0), pltpu.roll(b, 2, axis=0))))
    parts = [jnp.broadcast_to(b_ref[ci, m * 2 * s + s - 1:m * 2 * s + s, :], (2 * s, b.shape[1]))
             for m in range(c // (2 * s))]
    return parts[0] if len(parts) == 1 else jnp.concatenate(parts, axis=0)


def _scan_kernel(q_ref, k_ref, g_ref, v_ref, mask_ref, sign_ref, o_ref,
                 state_ref, b_scr, a_scr, u_scr, qb_scr, sc_scr):
    c = SCAN_CHUNK
    dk = q_ref.shape[-1]
    n_chunks = q_ref.shape[1] // c
    n_levels = len(SCAN_LEVELS)

    @pl.when(pl.program_id(2) == 0)
    def _():
        state_ref[...] = jnp.zeros_like(state_ref)

    row = lax.broadcasted_iota(jnp.int32, (c, dk), 0)

    def chunk_rows(ci):
        return pl.ds(pl.multiple_of(ci * c, c), c)

    def prepare(ci, carry):
        rows = chunk_rows(ci)
        q = q_ref[0, rows, :].astype(jnp.float32)
        k = k_ref[0, rows, :].astype(jnp.float32)
        v = v_ref[0, rows, :].astype(jnp.float32)
        b = _cumsum_rows(g_ref[0, rows, :] * LOG2_E, row)
        b_scr[ci] = b
        b_last = b[c - 1:c, :]
        a_scr[ci] = jnp.broadcast_to(jnp.exp2(b_last), (SUBLANES, dk))
        u_scr[ci] = _dot(v.T, k * jnp.exp2(b_last - b))
        qb_scr[ci] = (q * jnp.exp2(b)).astype(qb_scr.dtype)
        scores = _dot_nt(q, k).astype(MXU_DTYPE) * mask_ref[n_levels]
        for idx, s in enumerate(SCAN_LEVELS):
            mid = _level_reference(b, b_scr, ci, s, row)
            e = jnp.exp2((b - mid) * sign_ref[idx])
            scores = scores + _dot_nt(q * e, k * e).astype(MXU_DTYPE) * mask_ref[idx]
        sc_scr[ci] = scores
        return carry

    lax.fori_loop(0, n_chunks, prepare, 0, unroll=SCAN_UNROLL)

    def emit(ci, state_t):
        rows = chunk_rows(ci)
        o = _dot(sc_scr[ci], v_ref[0, rows, :]) + _dot_nt(qb_scr[ci], state_t)
        o_ref[0, rows, :] = o.astype(o_ref.dtype)
        return state_t * a_scr[ci, 0:1, :] + u_scr[ci]

    state_ref[...] = lax.fori_loop(0, n_chunks, emit, state_ref[...], unroll=SCAN_UNROLL)


def _chunk_scan(q, k, g, v, heads):
    bsz, seq, _ = q.shape
    dk = q.shape[-1] // heads
    dv = v.shape[-1] // heads
    n_chunks = SCAN_TILE // SCAN_CHUNK
    masks, signs = _level_constants(dk)
    spec = lambda d: pl.BlockSpec((1, SCAN_TILE, d), lambda b, h, t: (b, t, h))
    return pl.pallas_call(
        _scan_kernel,
        grid=(bsz, heads, seq // SCAN_TILE),
        in_specs=[spec(dk), spec(dk), spec(dk), spec(dv),
                  pl.BlockSpec(masks.shape, lambda b, h, t: (0, 0, 0)),
                  pl.BlockSpec(signs.shape, lambda b, h, t: (0, 0, 0))],
        out_specs=spec(dv),
        out_shape=jax.ShapeDtypeStruct(v.shape, jnp.float32),
        scratch_shapes=[pltpu.VMEM((dv, dk), jnp.float32),
                        pltpu.VMEM((n_chunks, SCAN_CHUNK, dk), jnp.float32),
                        pltpu.VMEM((n_chunks, SUBLANES, dk), jnp.float32),
                        pltpu.VMEM((n_chunks, dv, dk), jnp.float32),
                        pltpu.VMEM((n_chunks, SCAN_CHUNK, dk), MXU_DTYPE),
                        pltpu.VMEM((n_chunks, SCAN_CHUNK, SCAN_CHUNK), MXU_DTYPE)],
        compiler_params=pltpu.CompilerParams(dimension_semantics=("arbitrary", "arbitrary", "arbitrary"),
                                             vmem_limit_bytes=VMEM_LIMIT_BYTES),
        name="chunk_scan",
    )(q, k, g, v, masks, signs)


def _post_ffn_kernel(norm_width, apply_final, o_ref, gate_ref, x_ref, onorm_ref, wo_ref, fnorm_ref,
                     wg_ref, wu_ref, wd_ref, final_ref, y_ref):
    width = o_ref.shape[-1]
    mixed = None
    for lo in range(0, width, norm_width):
        seg = o_ref[:, lo:lo + norm_width]
        normed = _rms_norm(seg, onorm_ref[:, lo:lo + norm_width])
        gated = normed * _silu(gate_ref[:, lo:lo + norm_width].astype(jnp.float32))
        part = _dot(gated, wo_ref[lo:lo + norm_width, :])
        mixed = part if mixed is None else mixed + part
    x1 = x_ref[...] + mixed
    h = _rms_norm(x1, fnorm_ref[...]).astype(MXU_DTYPE)
    y = x1
    for lo in range(0, D_FF, FF_TILE):
        act = _silu(_dot(h, wg_ref[:, lo:lo + FF_TILE])) * _dot(h, wu_ref[:, lo:lo + FF_TILE])
        y = y + _dot(act, wd_ref[lo:lo + FF_TILE, :])
    if apply_final:
        y = _rms_norm(y, final_ref[...])
    y_ref[...] = y


def _post_ffn(o2, gate2, x2, out_norm, norm_width, w_out, ffn_norm, ffn_w_in, ffn_w_out, final_norm,
              apply_final):
    m = x2.shape[0]
    row = lambda: pl.BlockSpec((ROW_TILE, D_MODEL), lambda i: (i, 0))
    wg = ffn_w_in[:, :D_FF].astype(MXU_DTYPE)
    wu = ffn_w_in[:, D_FF:].astype(MXU_DTYPE)
    return pl.pallas_call(
        functools.partial(_post_ffn_kernel, norm_width, apply_final),
        grid=(m // ROW_TILE,),
        in_specs=[row(), row(), row(), _resident((1, D_MODEL)), _resident((D_MODEL, D_MODEL)),
                  _resident((1, D_MODEL)), _resident((D_MODEL, D_FF)), _resident((D_MODEL, D_FF)),
                  _resident((D_FF, D_MODEL)), _resident((1, D_MODEL))],
        out_specs=row(),
        out_shape=jax.ShapeDtypeStruct((m, D_MODEL), jnp.float32),
        compiler_params=pltpu.CompilerParams(dimension_semantics=("arbitrary",),
                                             vmem_limit_bytes=VMEM_LIMIT_BYTES),
        name="post_ffn",
    )(o2, gate2, x2, out_norm.reshape(1, -1), w_out.astype(MXU_DTYPE), ffn_norm.reshape(1, -1),
      wg, wu, ffn_w_out.astype(MXU_DTYPE), final_norm.reshape(1, -1))


def kernel(x, mixer_norm, ffn_norm, gla_w_in, gla_w_gate_up, gla_b_gate, gla_head_norm, gla_w_out,
           hgrn_w_in, hgrn_lower_bounds, hgrn_out_norm, hgrn_w_out, ffn_w_in, ffn_w_out, final_norm):
    bsz, seq, d = x.shape
    m = bsz * seq
    x2 = x.reshape(m, d)
    for layer in range(DEPTH):
        j = layer // N_MIXERS
        last = layer == DEPTH - 1
        if layer % N_MIXERS == 0:
            q, k, v, r, g = _gla_in_proj(x2, mixer_norm[layer], gla_w_in[j], gla_w_gate_up[j], gla_b_gate[j])
            shp = lambda t: t.reshape(bsz, seq, t.shape[-1])
            o = _chunk_scan(shp(q), shp(k), shp(g), shp(v), GLA_HEADS)
            x2 = _post_ffn(o.reshape(m, GLA_VAL), r, x2, jnp.tile(gla_head_norm[j], GLA_HEADS), GLA_DV,
                           gla_w_out[j], ffn_norm[layer], ffn_w_in[layer], ffn_w_out[layer], final_norm, last)
        else:
            q, k, lf, v, gate = _hgrn_in_proj(x2, mixer_norm[layer], hgrn_lower_bounds, layer, hgrn_w_in[j])
            shp = lambda t: t.reshape(bsz, seq, t.shape[-1])
            o = _chunk_scan(shp(q), shp(k), shp(lf), shp(v), HG_HEADS)
            x2 = _post_ffn(o.reshape(m, HG_VAL), gate, x2, hgrn_out_norm[j], HG_VAL,
                           hgrn_w_out[j], ffn_norm[layer], ffn_w_in[layer], ffn_w_out[layer], final_norm, last)
    return x2.reshape(bsz, seq, d)
```

```python
import functools
import math

import jax
import jax.numpy as jnp
from jax import lax
from jax.experimental import pallas as pl
from jax.experimental.pallas import tpu as pltpu

D_MODEL = 1024
DEPTH = 2
N_MIXERS = 2
NORM_EPS = 1e-6

GLA_HEADS = 4
GLA_KEY = D_MODEL // 2
GLA_VAL = D_MODEL
GLA_DK = GLA_KEY // GLA_HEADS
GLA_DV = GLA_VAL // GLA_HEADS
GLA_GATE_RANK = 16
GLA_GATE_NORMALIZER = 16.0

HG_EXPAND = 128
HG_HEADS = D_MODEL // HG_EXPAND
HG_DK = HG_EXPAND
HG_DV = D_MODEL // HG_HEADS
HG_KEY = HG_HEADS * HG_DK
HG_VAL = HG_HEADS * HG_DV

D_FF = int(math.ceil(8 * D_MODEL / 3 / 256)) * 256

LANES = 128
SUBLANES = 8
VMEM_LIMIT_BYTES = 56 * 1024 * 1024

SCAN_CHUNK = 64
SCAN_LEVELS = tuple(SCAN_CHUNK >> (i + 1) for i in range(int(math.log2(SCAN_CHUNK))))

ROW_TILE = 512
SCAN_TILE = 1024
FF_TILE = 1408
SCAN_UNROLL = 16

ACT_DTYPE = jnp.bfloat16
QK_DTYPE = jnp.float32
MXU_DTYPE = jnp.bfloat16
LOG2_E = 1.4426950408889634


def _dot(a, b):
    return jnp.dot(a.astype(MXU_DTYPE), b.astype(MXU_DTYPE), preferred_element_type=jnp.float32)


def _dot_nt(a, b):
    return lax.dot_general(a.astype(MXU_DTYPE), b.astype(MXU_DTYPE), (((1,), (1,)), ((), ())),
                           preferred_element_type=jnp.float32)


def _rms_norm(xf, gain):
    inv = lax.rsqrt(jnp.mean(xf * xf, axis=-1, keepdims=True) + NORM_EPS)
    return xf * inv * gain


def _log_sigmoid(z):
    return jnp.minimum(z, 0.0) - jnp.log(1.0 + jnp.exp(-jnp.abs(z)))


def _silu(z):
    return z * (1.0 / (1.0 + jnp.exp(-z)))


def _resident(shape):
    return pl.BlockSpec(shape, lambda *_: (0,) * len(shape), pipeline_mode=pl.Buffered(1))


def _gla_in_kernel(x_ref, gain_ref, wq_ref, wk_ref, wv_ref, wr_ref, wgd_ref, wgu_ref, bg_ref,
                   q_ref, k_ref, v_ref, r_ref, g_ref):
    h = _rms_norm(x_ref[...], gain_ref[...]).astype(MXU_DTYPE)
    q_ref[...] = (_dot(h, wq_ref[...]) * (GLA_DK ** -0.5)).astype(q_ref.dtype)
    k_ref[...] = _dot(h, wk_ref[...]).astype(k_ref.dtype)
    v_ref[...] = _dot(h, wv_ref[...]).astype(v_ref.dtype)
    r_ref[...] = _dot(h, wr_ref[...]).astype(r_ref.dtype)
    gd = _dot(h, wgd_ref[...])
    z = _dot(gd, wgu_ref[...]) + bg_ref[...]
    g_ref[...] = _log_sigmoid(z) / GLA_GATE_NORMALIZER


def _gla_in_proj(x2, gain, w_in, w_gate_up, b_gate):
    m = x2.shape[0]
    kk, vv = GLA_KEY, GLA_VAL
    wq = w_in[:, :kk].astype(MXU_DTYPE)
    wk = w_in[:, kk:2 * kk].astype(MXU_DTYPE)
    wv = w_in[:, 2 * kk:2 * kk + vv].astype(MXU_DTYPE)
    wr = w_in[:, 2 * kk + vv:2 * kk + 2 * vv].astype(MXU_DTYPE)
    pad = LANES - GLA_GATE_RANK
    wgd = jnp.pad(w_in[:, 2 * kk + 2 * vv:], ((0, 0), (0, pad))).astype(MXU_DTYPE)
    wgu = jnp.pad(w_gate_up, ((0, pad), (0, 0))).astype(MXU_DTYPE)
    row = lambda n: pl.BlockSpec((ROW_TILE, n), lambda i: (i, 0))
    return pl.pallas_call(
        _gla_in_kernel,
        grid=(m // ROW_TILE,),
        in_specs=[row(D_MODEL), _resident((1, D_MODEL)),
                  _resident((D_MODEL, kk)), _resident((D_MODEL, kk)),
                  _resident((D_MODEL, vv)), _resident((D_MODEL, vv)),
                  _resident((D_MODEL, LANES)), _resident((LANES, kk)), _resident((1, kk))],
        out_specs=[row(kk), row(kk), row(vv), row(vv), row(kk)],
        out_shape=[jax.ShapeDtypeStruct((m, kk), QK_DTYPE), jax.ShapeDtypeStruct((m, kk), QK_DTYPE),
                   jax.ShapeDtypeStruct((m, vv), ACT_DTYPE), jax.ShapeDtypeStruct((m, vv), ACT_DTYPE),
                   jax.ShapeDtypeStruct((m, kk), jnp.float32)],
        compiler_params=pltpu.CompilerParams(dimension_semantics=("arbitrary",),
                                             vmem_limit_bytes=VMEM_LIMIT_BYTES),
        name="gla_in_proj",
    )(x2, gain.reshape(1, -1), wq, wk, wv, wr, wgd, wgu, b_gate.reshape(1, -1))


def _hgrn_in_kernel(layer, x_ref, gain_ref, lbp_ref, wq_ref, wf_ref, wi_ref, wg_ref,
                    q_ref, k_ref, lf_ref, v_ref, gate_ref):
    h = _rms_norm(x_ref[...], gain_ref[...]).astype(MXU_DTYPE)
    lbp = lbp_ref[...]
    e = jnp.exp(lbp - jnp.max(lbp, axis=0, keepdims=True))
    soft = e / jnp.sum(e, axis=0, keepdims=True)
    lb = jnp.sum(soft[:layer + 1], axis=0, keepdims=True) - soft[0:1]
    f = _dot(h, wf_ref[...])
    q = _dot(h, wq_ref[...])
    e = jnp.exp(-jnp.abs(f))
    one_plus_e = 1.0 + e
    a = jnp.log(lb)
    c = jnp.log1p(-lb) + (jnp.minimum(f, 0.0) - jnp.log(one_plus_e))
    lf_ref[...] = jnp.maximum(a, c) + jnp.log(1.0 + jnp.exp(-jnp.abs(a - c)))
    k_ref[...] = ((1.0 - lb) * jnp.where(f >= 0.0, e, 1.0) * (1.0 / one_plus_e)).astype(k_ref.dtype)
    v_ref[...] = _dot(h, wi_ref[...]).astype(v_ref.dtype)
    q_ref[...] = (_silu(q) * (HG_DK ** -0.5)).astype(q_ref.dtype)
    gate_ref[...] = _dot(h, wg_ref[...]).astype(gate_ref.dtype)


def _hgrn_in_proj(x2, gain, lower_bounds, layer, w_in):
    m = x2.shape[0]
    n = HG_KEY
    ws = [w_in[:, i * n:(i + 1) * n].astype(MXU_DTYPE) for i in range(4)]
    row = lambda: pl.BlockSpec((ROW_TILE, n), lambda i: (i, 0))
    return pl.pallas_call(
        functools.partial(_hgrn_in_kernel, layer),
        grid=(m // ROW_TILE,),
        in_specs=[pl.BlockSpec((ROW_TILE, D_MODEL), lambda i: (i, 0)), _resident((1, D_MODEL)),
                  _resident((DEPTH, n))] + [_resident((D_MODEL, n))] * 4,
        out_specs=[row(), row(), row(), row(), row()],
        out_shape=[jax.ShapeDtypeStruct((m, n), QK_DTYPE), jax.ShapeDtypeStruct((m, n), QK_DTYPE),
                   jax.ShapeDtypeStruct((m, n), jnp.float32), jax.ShapeDtypeStruct((m, n), ACT_DTYPE),
                   jax.ShapeDtypeStruct((m, n), ACT_DTYPE)],
        compiler_params=pltpu.CompilerParams(dimension_semantics=("arbitrary",),
                                             vmem_limit_bytes=VMEM_LIMIT_BYTES),
        name="hgrn_in_proj",
    )(x2, gain.reshape(1, -1), lower_bounds, *ws)


def _level_constants(dk):
    c = SCAN_CHUNK
    i = jnp.arange(c)[:, None]
    j = jnp.arange(c)[None, :]
    masks = [(i // (2 * s) == j // (2 * s)) & (i % (2 * s) >= s) & (j % (2 * s) < s) for s in SCAN_LEVELS]
    masks.append(i == j)
    small = [s for s in SCAN_LEVELS if s < SUBLANES]
    signs = [jnp.where(jnp.arange(SUBLANES) % (2 * s) >= s, 1.0, -1.0) for s in small]
    signs = jnp.broadcast_to(jnp.stack(signs)[:, :, None], (len(small), SUBLANES, dk))
    tri3 = jnp.tile(i >= j, (1, 3))
    return jnp.stack(masks).astype(MXU_DTYPE), signs.astype(jnp.float32), tri3.astype(MXU_DTYPE)


def _cumsum_chunks(g, tri3):
    c = SCAN_CHUNK
    hi = g.astype(MXU_DTYPE)
    rest = g - hi.astype(jnp.float32)
    mid = rest.astype(MXU_DTYPE)
    lo = (rest - mid.astype(jnp.float32)).astype(MXU_DTYPE)
    cols = [jnp.concatenate([t[r:r + c, :] for t in (hi, mid, lo)], axis=0) for r in range(0, g.shape[0], c)]
    return jnp.dot(tri3, jnp.concatenate(cols, axis=1), preferred_element_type=jnp.float32)


def _level_factors(b, b_ref, ci, s, row, sign_ref):
    c, dk = b.shape
    tile = lambda r: jnp.broadcast_to(b_ref[ci, r:r + 1, :], (SUBLANES, dk))
    pieces = []
    for lo in range(0, c, SUBLANES):
        piece = b[lo:lo + SUBLANES, :]
        if s >= SUBLANES:
            mid = tile((lo // (2 * s)) * 2 * s + s - 1)
            d = piece - mid if lo % (2 * s) >= s else mid - piece
        else:
            if s == 1:
                mid = jnp.where(row[:SUBLANES] % 2 == 1, pltpu.roll(piece, 1, axis=0), piece)
            else:
                mid = tile(lo + s - 1)
                for blk in range(1, SUBLANES // (2 * s)):
                    mid = jnp.where(row[:SUBLANES] >= blk * 2 * s, tile(lo + blk * 2 * s + s - 1), mid)
            d = (piece - mid) * sign_ref[[t for t in SCAN_LEVELS if t < SUBLANES].index(s), :SUBLANES, :]
        pieces.append(jnp.exp2(d))
    return pieces


def _scan_kernel(q_ref, k_ref, g_ref, v_ref, mask_ref, sign_ref, tri_ref, o_ref,
                 state_ref, b_scr, a_scr, u_scr, qb_scr, sc_scr):
    c = SCAN_CHUNK
    dk = q_ref.shape[-1]
    n_chunks = q_ref.shape[1] // c
    n_levels = len(SCAN_LEVELS)

    @pl.when(pl.program_id(2) == 0)
    def _():
        state_ref[...] = jnp.zeros_like(state_ref)

    row = lax.broadcasted_iota(jnp.int32, (c, dk), 0)

    def chunk_rows(ci):
        return pl.ds(pl.multiple_of(ci * c, c), c)

    b_all = _cumsum_chunks(g_ref[0] * LOG2_E, tri_ref[...])
    for ci in range(n_chunks):
        b_scr[ci] = b_all[:, ci * dk:(ci + 1) * dk]

    def prepare(ci, carry):
        rows = chunk_rows(ci)
        q = q_ref[0, rows, :].astype(jnp.float32)
        k = k_ref[0, rows, :].astype(jnp.float32)
        b = b_scr[ci]
        b_last = jnp.broadcast_to(b_scr[ci, c - 1:c, :], (SUBLANES, dk))
        decay = jnp.concatenate([jnp.exp2(b_last)] * (LANES // SUBLANES), axis=0).T
        a_scr[ci] = jnp.concatenate([decay] * (a_scr.shape[-1] // LANES), axis=1)
        k_dec = k * jnp.exp2(jnp.concatenate([b_last] * (c // SUBLANES), axis=0) - b)
        u_scr[ci] = _dot(k_dec.T, v_ref[0, rows, :])
        qb_scr[ci] = (q * jnp.exp2(b)).astype(qb_scr.dtype)
        scores = _dot_nt(q, k).astype(MXU_DTYPE) * mask_ref[n_levels]
        for idx, s in enumerate(SCAN_LEVELS):
            e = _level_factors(b, b_scr, ci, s, row, sign_ref)
            if s >= SUBLANES:
                z = jnp.concatenate([(q if lo % (2 * s) >= s else k)[lo:lo + SUBLANES, :] * e[lo // SUBLANES]
                                     for lo in range(0, c, SUBLANES)], axis=0)
                level = _dot_nt(z, z)
            else:
                e = jnp.concatenate(e, axis=0)
                level = _dot_nt(q * e, k * e)
            scores = scores + level.astype(MXU_DTYPE) * mask_ref[idx]
        sc_scr[ci] = scores
        return carry

    lax.fori_loop(0, n_chunks, prepare, 0, unroll=SCAN_UNROLL)

    def emit(ci, state):
        rows = chunk_rows(ci)
        o = _dot(sc_scr[ci], v_ref[0, rows, :]) + _dot(qb_scr[ci], state)
        o_ref[0, rows, :] = o.astype(o_ref.dtype)
        return state * a_scr[ci] + u_scr[ci]

    state_ref[...] = lax.fori_loop(0, n_chunks, emit, state_ref[...], unroll=SCAN_UNROLL)


def _chunk_scan(q, k, g, v, heads):
    bsz, seq, _ = q.shape
    dk = q.shape[-1] // heads
    dv = v.shape[-1] // heads
    n_chunks = SCAN_TILE // SCAN_CHUNK
    masks, signs, tri3 = _level_constants(dk)
    spec = lambda d: pl.BlockSpec((1, SCAN_TILE, d), lambda b, h, t: (b, t, h))
    return pl.pallas_call(
        _scan_kernel,
        grid=(bsz, heads, seq // SCAN_TILE),
        in_specs=[spec(dk), spec(dk), spec(dk), spec(dv),
                  pl.BlockSpec(masks.shape, lambda b, h, t: (0, 0, 0)),
                  pl.BlockSpec(signs.shape, lambda b, h, t: (0, 0, 0)),
                  pl.BlockSpec(tri3.shape, lambda b, h, t: (0, 0))],
        out_specs=spec(dv),
        out_shape=jax.ShapeDtypeStruct(v.shape, jnp.float32),
        scratch_shapes=[pltpu.VMEM((dk, dv), jnp.float32),
                        pltpu.VMEM((n_chunks, SCAN_CHUNK, dk), jnp.float32),
                        pltpu.VMEM((n_chunks, dk, dv), jnp.float32),
                        pltpu.VMEM((n_chunks, dk, dv), jnp.float32),
                        pltpu.VMEM((n_chunks, SCAN_CHUNK, dk), MXU_DTYPE),
                        pltpu.VMEM((n_chunks, SCAN_CHUNK, SCAN_CHUNK), MXU_DTYPE)],
        compiler_params=pltpu.CompilerParams(dimension_semantics=("arbitrary", "arbitrary", "arbitrary"),
                                             vmem_limit_bytes=VMEM_LIMIT_BYTES),
        name="chunk_scan",
    )(q, k, g, v, masks, signs, tri3)


def _post_ffn_kernel(norm_width, apply_final, o_ref, gate_ref, x_ref, onorm_ref, wo_ref, fnorm_ref,
                     wg_ref, wu_ref, wd_ref, final_ref, y_ref):
    width = o_ref.shape[-1]
    mixed = None
    for lo in range(0, width, norm_width):
        seg = o_ref[:, lo:lo + norm_width]
        normed = _rms_norm(seg, onorm_ref[:, lo:lo + norm_width])
        gated = normed * _silu(gate_ref[:, lo:lo + norm_width].astype(jnp.float32))
        part = _dot(gated, wo_ref[lo:lo + norm_width, :])
        mixed = part if mixed is None else mixed + part
    x1 = x_ref[...] + mixed
    h = _rms_norm(x1, fnorm_ref[...]).astype(MXU_DTYPE)
    y = x1
    for lo in range(0, D_FF, FF_TILE):
        act = _silu(_dot(h, wg_ref[:, lo:lo + FF_TILE])) * _dot(h, wu_ref[:, lo:lo + FF_TILE])
        y = y + _dot(act, wd_ref[lo:lo + FF_TILE, :])
    if apply_final:
        y = _rms_norm(y, final_ref[...])
    y_ref[...] = y


def _post_ffn(o2, gate2, x2, out_norm, norm_width, w_out, ffn_norm, ffn_w_in, ffn_w_out, final_norm,
              apply_final):
    m = x2.shape[0]
    row = lambda: pl.BlockSpec((ROW_TILE, D_MODEL), lambda i: (i, 0))
    wg = ffn_w_in[:, :D_FF].astype(MXU_DTYPE)
    wu = ffn_w_in[:, D_FF:].astype(MXU_DTYPE)
    return pl.pallas_call(
        functools.partial(_post_ffn_kernel, norm_width, apply_final),
        grid=(m // ROW_TILE,),
        in_specs=[row(), row(), row(), _resident((1, D_MODEL)), _resident((D_MODEL, D_MODEL)),
                  _resident((1, D_MODEL)), _resident((D_MODEL, D_FF)), _resident((D_MODEL, D_FF)),
                  _resident((D_FF, D_MODEL)), _resident((1, D_MODEL))],
        out_specs=row(),
        out_shape=jax.ShapeDtypeStruct((m, D_MODEL), jnp.float32),
        compiler_params=pltpu.CompilerParams(dimension_semantics=("arbitrary",),
                                             vmem_limit_bytes=VMEM_LIMIT_BYTES),
        name="post_ffn",
    )(o2, gate2, x2, out_norm.reshape(1, -1), w_out.astype(MXU_DTYPE), ffn_norm.reshape(1, -1),
      wg, wu, ffn_w_out.astype(MXU_DTYPE), final_norm.reshape(1, -1))


def kernel(x, mixer_norm, ffn_norm, gla_w_in, gla_w_gate_up, gla_b_gate, gla_head_norm, gla_w_out,
           hgrn_w_in, hgrn_lower_bounds, hgrn_out_norm, hgrn_w_out, ffn_w_in, ffn_w_out, final_norm):
    bsz, seq, d = x.shape
    m = bsz * seq
    x2 = x.reshape(m, d)
    for layer in range(DEPTH):
        j = layer // N_MIXERS
        last = layer == DEPTH - 1
        if layer % N_MIXERS == 0:
            q, k, v, r, g = _gla_in_proj(x2, mixer_norm[layer], gla_w_in[j], gla_w_gate_up[j], gla_b_gate[j])
            shp = lambda t: t.reshape(bsz, seq, t.shape[-1])
            o = _chunk_scan(shp(q), shp(k), shp(g), shp(v), GLA_HEADS)
            x2 = _post_ffn(o.reshape(m, GLA_VAL), r, x2, jnp.tile(gla_head_norm[j], GLA_HEADS), GLA_DV,
                           gla_w_out[j], ffn_norm[layer], ffn_w_in[layer], ffn_w_out[layer], final_norm, last)
        else:
            q, k, lf, v, gate = _hgrn_in_proj(x2, mixer_norm[layer], hgrn_lower_bounds, layer, hgrn_w_in[j])
            shp = lambda t: t.reshape(bsz, seq, t.shape[-1])
            o = _chunk_scan(shp(q), shp(k), shp(lf), shp(v), HG_HEADS)
            x2 = _post_ffn(o.reshape(m, HG_VAL), gate, x2, hgrn_out_norm[j], HG_VAL,
                           hgrn_w_out[j], ffn_norm[layer], ffn_w_in[layer], ffn_w_out[layer], final_norm, last)
    return x2.reshape(bsz, seq, d)
```

```python
import functools
import math

import jax
import jax.numpy as jnp
from jax import lax
from jax.experimental import pallas as pl
from jax.experimental.pallas import tpu as pltpu

D_MODEL = 1024
DEPTH = 2
N_MIXERS = 2
NORM_EPS = 1e-6

GLA_HEADS = 4
GLA_KEY = D_MODEL // 2
GLA_VAL = D_MODEL
GLA_DK = GLA_KEY // GLA_HEADS
GLA_DV = GLA_VAL // GLA_HEADS
GLA_GATE_RANK = 16
GLA_GATE_NORMALIZER = 16.0

HG_EXPAND = 128
HG_HEADS = D_MODEL // HG_EXPAND
HG_DK = HG_EXPAND
HG_DV = D_MODEL // HG_HEADS
HG_KEY = HG_HEADS * HG_DK
HG_VAL = HG_HEADS * HG_DV

D_FF = int(math.ceil(8 * D_MODEL / 3 / 256)) * 256

LANES = 128
SUBLANES = 8
VMEM_LIMIT_BYTES = 56 * 1024 * 1024

SCAN_CHUNK = 64
SCAN_LEVELS = tuple(SCAN_CHUNK >> (i + 1) for i in range(int(math.log2(SCAN_CHUNK))))

ROW_TILE = 512
SCAN_TILE = 4096
FF_TILE = 1408
MIX_ROWS = 128
SCAN_UNROLL = 64

ACT_DTYPE = jnp.bfloat16
QK_DTYPE = jnp.float32
MXU_DTYPE = jnp.bfloat16
LOG2_E = 1.4426950408889634


def _dot(a, b):
    return jnp.dot(a.astype(MXU_DTYPE), b.astype(MXU_DTYPE), preferred_element_type=jnp.float32)


def _dot_nt(a, b):
    return lax.dot_general(a.astype(MXU_DTYPE), b.astype(MXU_DTYPE), (((1,), (1,)), ((), ())),
                           preferred_element_type=jnp.float32)


def _rms_norm(xf, gain):
    inv = lax.rsqrt(jnp.mean(xf * xf, axis=-1, keepdims=True) + NORM_EPS)
    return xf * inv * gain


def _log_sigmoid(z):
    return jnp.minimum(z, 0.0) - jnp.log(1.0 + jnp.exp(-jnp.abs(z)))


def _silu(z):
    return z * (1.0 / (1.0 + jnp.exp(-z)))


def _resident(shape):
    return pl.BlockSpec(shape, lambda *_: (0,) * len(shape), pipeline_mode=pl.Buffered(1))


def _gla_in_kernel(x_ref, gain_ref, wq_ref, wk_ref, wv_ref, wr_ref, wgd_ref, wgu_ref, bg_ref,
                   q_ref, k_ref, v_ref, r_ref, g_ref):
    h = _rms_norm(x_ref[...], gain_ref[...]).astype(MXU_DTYPE)
    q_ref[...] = (_dot(h, wq_ref[...]) * (GLA_DK ** -0.5)).astype(q_ref.dtype)
    k_ref[...] = _dot(h, wk_ref[...]).astype(k_ref.dtype)
    v_ref[...] = _dot(h, wv_ref[...]).astype(v_ref.dtype)
    r_ref[...] = _dot(h, wr_ref[...]).astype(r_ref.dtype)
    gd = _dot(h, wgd_ref[...])
    z = _dot(gd, wgu_ref[...]) + bg_ref[...]
    g_ref[...] = _log_sigmoid(z) / GLA_GATE_NORMALIZER


def _gla_in_proj(x2, gain, w_in, w_gate_up, b_gate):
    m = x2.shape[0]
    kk, vv = GLA_KEY, GLA_VAL
    wq = w_in[:, :kk].astype(MXU_DTYPE)
    wk = w_in[:, kk:2 * kk].astype(MXU_DTYPE)
    wv = w_in[:, 2 * kk:2 * kk + vv].astype(MXU_DTYPE)
    wr = w_in[:, 2 * kk + vv:2 * kk + 2 * vv].astype(MXU_DTYPE)
    pad = LANES - GLA_GATE_RANK
    wgd = jnp.pad(w_in[:, 2 * kk + 2 * vv:], ((0, 0), (0, pad))).astype(MXU_DTYPE)
    wgu = jnp.pad(w_gate_up, ((0, pad), (0, 0))).astype(MXU_DTYPE)
    row = lambda n: pl.BlockSpec((ROW_TILE, n), lambda i: (i, 0))
    return pl.pallas_call(
        _gla_in_kernel,
        grid=(m // ROW_TILE,),
        in_specs=[row(D_MODEL), _resident((1, D_MODEL)),
                  _resident((D_MODEL, kk)), _resident((D_MODEL, kk)),
                  _resident((D_MODEL, vv)), _resident((D_MODEL, vv)),
                  _resident((D_MODEL, LANES)), _resident((LANES, kk)), _resident((1, kk))],
        out_specs=[row(kk), row(kk), row(vv), row(vv), row(kk)],
        out_shape=[jax.ShapeDtypeStruct((m, kk), QK_DTYPE), jax.ShapeDtypeStruct((m, kk), QK_DTYPE),
                   jax.ShapeDtypeStruct((m, vv), ACT_DTYPE), jax.ShapeDtypeStruct((m, vv), ACT_DTYPE),
                   jax.ShapeDtypeStruct((m, kk), jnp.float32)],
        compiler_params=pltpu.CompilerParams(dimension_semantics=("arbitrary",),
                                             vmem_limit_bytes=VMEM_LIMIT_BYTES),
        name="gla_in_proj",
    )(x2, gain.reshape(1, -1), wq, wk, wv, wr, wgd, wgu, b_gate.reshape(1, -1))


def _hgrn_in_kernel(layer, x_ref, gain_ref, lbp_ref, wq_ref, wf_ref, wi_ref, wg_ref,
                    q_ref, k_ref, lf_ref, v_ref, gate_ref):
    h = _rms_norm(x_ref[...], gain_ref[...]).astype(MXU_DTYPE)
    lbp = lbp_ref[...]
    e = jnp.exp(lbp - jnp.max(lbp, axis=0, keepdims=True))
    soft = e / jnp.sum(e, axis=0, keepdims=True)
    lb = jnp.sum(soft[:layer + 1], axis=0, keepdims=True) - soft[0:1]
    f = _dot(h, wf_ref[...])
    q = _dot(h, wq_ref[...])
    e = jnp.exp(-jnp.abs(f))
    one_plus_e = 1.0 + e
    a = jnp.log(lb)
    c = jnp.log1p(-lb) + (jnp.minimum(f, 0.0) - jnp.log(one_plus_e))
    lf_ref[...] = jnp.maximum(a, c) + jnp.log(1.0 + jnp.exp(-jnp.abs(a - c)))
    k_ref[...] = ((1.0 - lb) * jnp.where(f >= 0.0, e, 1.0) * (1.0 / one_plus_e)).astype(k_ref.dtype)
    v_ref[...] = _dot(h, wi_ref[...]).astype(v_ref.dtype)
    q_ref[...] = (_silu(q) * (HG_DK ** -0.5)).astype(q_ref.dtype)
    gate_ref[...] = _dot(h, wg_ref[...]).astype(gate_ref.dtype)


def _hgrn_in_proj(x2, gain, lower_bounds, layer, w_in):
    m = x2.shape[0]
    n = HG_KEY
    ws = [w_in[:, i * n:(i + 1) * n].astype(MXU_DTYPE) for i in range(4)]
    row = lambda: pl.BlockSpec((ROW_TILE, n), lambda i: (i, 0))
    return pl.pallas_call(
        functools.partial(_hgrn_in_kernel, layer),
        grid=(m // ROW_TILE,),
        in_specs=[pl.BlockSpec((ROW_TILE, D_MODEL), lambda i: (i, 0)), _resident((1, D_MODEL)),
                  _resident((DEPTH, n))] + [_resident((D_MODEL, n))] * 4,
        out_specs=[row(), row(), row(), row(), row()],
        out_shape=[jax.ShapeDtypeStruct((m, n), QK_DTYPE), jax.ShapeDtypeStruct((m, n), QK_DTYPE),
                   jax.ShapeDtypeStruct((m, n), jnp.float32), jax.ShapeDtypeStruct((m, n), ACT_DTYPE),
                   jax.ShapeDtypeStruct((m, n), ACT_DTYPE)],
        compiler_params=pltpu.CompilerParams(dimension_semantics=("arbitrary",),
                                             vmem_limit_bytes=VMEM_LIMIT_BYTES),
        name="hgrn_in_proj",
    )(x2, gain.reshape(1, -1), lower_bounds, *ws)


def _level_constants(dk):
    c = SCAN_CHUNK
    i = jnp.arange(c)[:, None]
    j = jnp.arange(c)[None, :]
    masks = [(i // (2 * s) == j // (2 * s)) & (i % (2 * s) >= s) & (j % (2 * s) < s) for s in SCAN_LEVELS]
    masks.append(i == j)
    small = [s for s in SCAN_LEVELS if s < SUBLANES]
    signs = [jnp.where(jnp.arange(SUBLANES) % (2 * s) >= s, 1.0, -1.0) for s in small]
    signs = jnp.broadcast_to(jnp.stack(signs)[:, :, None], (len(small), SUBLANES, dk))
    tri3 = jnp.tile(i >= j, (1, 3))
    return jnp.stack(masks).astype(MXU_DTYPE), signs.astype(jnp.float32), tri3.astype(MXU_DTYPE)


def _cumsum_chunks(g, tri3):
    c = SCAN_CHUNK
    hi = g.astype(MXU_DTYPE)
    rest = g - hi.astype(jnp.float32)
    mid = rest.astype(MXU_DTYPE)
    lo = (rest - mid.astype(jnp.float32)).astype(MXU_DTYPE)
    cols = [jnp.concatenate([t[r:r + c, :] for t in (hi, mid, lo)], axis=0) for r in range(0, g.shape[0], c)]
    return jnp.dot(tri3, jnp.concatenate(cols, axis=1), preferred_element_type=jnp.float32)


def _level_factors(b, b_ref, ci, s, row, sign_ref):
    c, dk = b.shape
    tile = lambda r: jnp.broadcast_to(b_ref[ci, r:r + 1, :], (SUBLANES, dk))
    pieces = []
    for lo in range(0, c, SUBLANES):
        piece = b[lo:lo + SUBLANES, :]
        if s >= SUBLANES:
            mid = tile((lo // (2 * s)) * 2 * s + s - 1)
            d = piece - mid if lo % (2 * s) >= s else mid - piece
        else:
            if s == 1:
                mid = jnp.where(row[:SUBLANES] % 2 == 1, pltpu.roll(piece, 1, axis=0), piece)
            else:
                mid = tile(lo + s - 1)
                for blk in range(1, SUBLANES // (2 * s)):
                    mid = jnp.where(row[:SUBLANES] >= blk * 2 * s, tile(lo + blk * 2 * s + s - 1), mid)
            d = (piece - mid) * sign_ref[[t for t in SCAN_LEVELS if t < SUBLANES].index(s), :SUBLANES, :]
        pieces.append(jnp.exp2(d))
    return pieces


def _scan_kernel(q_ref, k_ref, g_ref, v_ref, mask_ref, sign_ref, tri_ref, o_ref,
                 state_ref, b_scr, a_scr, u_scr, qb_scr, sc_scr):
    c = SCAN_CHUNK
    dk = q_ref.shape[-1]
    n_chunks = q_ref.shape[1] // c
    n_levels = len(SCAN_LEVELS)

    @pl.when(pl.program_id(2) == 0)
    def _():
        state_ref[...] = jnp.zeros_like(state_ref)

    row = lax.broadcasted_iota(jnp.int32, (c, dk), 0)

    def chunk_rows(ci):
        return pl.ds(pl.multiple_of(ci * c, c), c)

    b_all = _cumsum_chunks(g_ref[0] * LOG2_E, tri_ref[...])
    for ci in range(n_chunks):
        b_scr[ci] = b_all[:, ci * dk:(ci + 1) * dk]

    def prepare(ci, carry):
        rows = chunk_rows(ci)
        q = q_ref[0, rows, :].astype(jnp.float32)
        k = k_ref[0, rows, :].astype(jnp.float32)
        b = b_scr[ci]
        b_last = jnp.broadcast_to(b_scr[ci, c - 1:c, :], (SUBLANES, dk))
        decay = jnp.concatenate([jnp.exp2(b_last)] * (LANES // SUBLANES), axis=0).T
        a_scr[ci] = jnp.concatenate([decay] * (a_scr.shape[-1] // LANES), axis=1)
        k_dec = k * jnp.exp2(jnp.concatenate([b_last] * (c // SUBLANES), axis=0) - b)
        u_scr[ci] = _dot(k_dec.T, v_ref[0, rows, :])
        qb_scr[ci] = (q * jnp.exp2(b)).astype(qb_scr.dtype)
        scores = _dot_nt(q, k).astype(MXU_DTYPE) * mask_ref[n_levels]
        for idx, s in enumerate(SCAN_LEVELS):
            e = _level_factors(b, b_scr, ci, s, row, sign_ref)
            if s >= SUBLANES:
                z = jnp.concatenate([(q if lo % (2 * s) >= s else k)[lo:lo + SUBLANES, :] * e[lo // SUBLANES]
                                     for lo in range(0, c, SUBLANES)], axis=0)
                level = _dot_nt(z, z)
            else:
                e = jnp.concatenate(e, axis=0)
                level = _dot_nt(q * e, k * e)
            scores = scores + level.astype(MXU_DTYPE) * mask_ref[idx]
        sc_scr[ci] = scores
        return carry

    lax.fori_loop(0, n_chunks, prepare, 0, unroll=SCAN_UNROLL)

    def emit(ci, state):
        rows = chunk_rows(ci)
        o = _dot(sc_scr[ci], v_ref[0, rows, :]) + _dot(qb_scr[ci], state)
        o_ref[0, rows, :] = o.astype(o_ref.dtype)
        return state * a_scr[ci] + u_scr[ci]

    state_ref[...] = lax.fori_loop(0, n_chunks, emit, state_ref[...], unroll=SCAN_UNROLL)


def _chunk_scan(q, k, g, v, heads):
    bsz, seq, _ = q.shape
    dk = q.shape[-1] // heads
    dv = v.shape[-1] // heads
    n_chunks = SCAN_TILE // SCAN_CHUNK
    masks, signs, tri3 = _level_constants(dk)
    spec = lambda d: pl.BlockSpec((1, SCAN_TILE, d), lambda b, h, t: (b, t, h))
    return pl.pallas_call(
        _scan_kernel,
        grid=(bsz, heads, seq // SCAN_TILE),
        in_specs=[spec(dk), spec(dk), spec(dk), spec(dv),
                  pl.BlockSpec(masks.shape, lambda b, h, t: (0, 0, 0)),
                  pl.BlockSpec(signs.shape, lambda b, h, t: (0, 0, 0)),
                  pl.BlockSpec(tri3.shape, lambda b, h, t: (0, 0))],
        out_specs=spec(dv),
        out_shape=jax.ShapeDtypeStruct(v.shape, jnp.float32),
        scratch_shapes=[pltpu.VMEM((dk, dv), jnp.float32),
                        pltpu.VMEM((n_chunks, SCAN_CHUNK, dk), jnp.float32),
                        pltpu.VMEM((n_chunks, dk, dv), jnp.float32),
                        pltpu.VMEM((n_chunks, dk, dv), jnp.float32),
                        pltpu.VMEM((n_chunks, SCAN_CHUNK, dk), MXU_DTYPE),
                        pltpu.VMEM((n_chunks, SCAN_CHUNK, SCAN_CHUNK), MXU_DTYPE)],
        compiler_params=pltpu.CompilerParams(dimension_semantics=("arbitrary", "arbitrary", "arbitrary"),
                                             vmem_limit_bytes=VMEM_LIMIT_BYTES),
        name="chunk_scan",
    )(q, k, g, v, masks, signs, tri3)


def _post_ffn_kernel(norm_width, apply_final, o_ref, gate_ref, x_ref, onorm_ref, wo_ref, fnorm_ref,
                     wg_ref, wu_ref, wd_ref, final_ref, y_ref):
    width = o_ref.shape[-1]

    def mix(rows):
        mixed = None
        for lo in range(0, width, norm_width):
            seg = o_ref[rows, lo:lo + norm_width]
            normed = _rms_norm(seg, onorm_ref[:, lo:lo + norm_width])
            gated = normed * _silu(gate_ref[rows, lo:lo + norm_width].astype(jnp.float32))
            part = _dot(gated, wo_ref[lo:lo + norm_width, :])
            mixed = part if mixed is None else mixed + part
        x1 = x_ref[rows, :] + mixed
        return x1, _rms_norm(x1, fnorm_ref[...]).astype(MXU_DTYPE)

    parts = [mix(slice(r, r + MIX_ROWS)) for r in range(0, o_ref.shape[0], MIX_ROWS)]
    y = jnp.concatenate([p[0] for p in parts], axis=0)
    h = jnp.concatenate([p[1] for p in parts], axis=0)
    for lo in range(0, D_FF, FF_TILE):
        act = _silu(_dot(h, wg_ref[:, lo:lo + FF_TILE])) * _dot(h, wu_ref[:, lo:lo + FF_TILE])
        y = y + _dot(act, wd_ref[lo:lo + FF_TILE, :])
    if apply_final:
        y = _rms_norm(y, final_ref[...])
    y_ref[...] = y


def _post_ffn(o2, gate2, x2, out_norm, norm_width, w_out, ffn_norm, ffn_w_in, ffn_w_out, final_norm,
              apply_final):
    m = x2.shape[0]
    row = lambda: pl.BlockSpec((ROW_TILE, D_MODEL), lambda i: (i, 0))
    wg = ffn_w_in[:, :D_FF].astype(MXU_DTYPE)
    wu = ffn_w_in[:, D_FF:].astype(MXU_DTYPE)
    return pl.pallas_call(
        functools.partial(_post_ffn_kernel, norm_width, apply_final),
        grid=(m // ROW_TILE,),
        in_specs=[row(), row(), row(), _resident((1, D_MODEL)), _resident((D_MODEL, D_MODEL)),
                  _resident((1, D_MODEL)), _resident((D_MODEL, D_FF)), _resident((D_MODEL, D_FF)),
                  _resident((D_FF, D_MODEL)), _resident((1, D_MODEL))],
        out_specs=row(),
        out_shape=jax.ShapeDtypeStruct((m, D_MODEL), jnp.float32),
        compiler_params=pltpu.CompilerParams(dimension_semantics=("arbitrary",),
                                             vmem_limit_bytes=VMEM_LIMIT_BYTES),
        name="post_ffn",
    )(o2, gate2, x2, out_norm.reshape(1, -1), w_out.astype(MXU_DTYPE), ffn_norm.reshape(1, -1),
      wg, wu, ffn_w_out.astype(MXU_DTYPE), final_norm.reshape(1, -1))


def kernel(x, mixer_norm, ffn_norm, gla_w_in, gla_w_gate_up, gla_b_gate, gla_head_norm, gla_w_out,
           hgrn_w_in, hgrn_lower_bounds, hgrn_out_norm, hgrn_w_out, ffn_w_in, ffn_w_out, final_norm):
    bsz, seq, d = x.shape
    m = bsz * seq
    x2 = x.reshape(m, d)
    for layer in range(DEPTH):
        j = layer // N_MIXERS
        last = layer == DEPTH - 1
        if layer % N_MIXERS == 0:
            q, k, v, r, g = _gla_in_proj(x2, mixer_norm[layer], gla_w_in[j], gla_w_gate_up[j], gla_b_gate[j])
            shp = lambda t: t.reshape(bsz, seq, t.shape[-1])
            o = _chunk_scan(shp(q), shp(k), shp(g), shp(v), GLA_HEADS)
            x2 = _post_ffn(o.reshape(m, GLA_VAL), r, x2, jnp.tile(gla_head_norm[j], GLA_HEADS), GLA_DV,
                           gla_w_out[j], ffn_norm[layer], ffn_w_in[layer], ffn_w_out[layer], final_norm, last)
        else:
            q, k, lf, v, gate = _hgrn_in_proj(x2, mixer_norm[layer], hgrn_lower_bounds, layer, hgrn_w_in[j])
            shp = lambda t: t.reshape(bsz, seq, t.shape[-1])
            o = _chunk_scan(shp(q), shp(k), shp(lf), shp(v), HG_HEADS)
            x2 = _post_ffn(o.reshape(m, HG_VAL), gate, x2, hgrn_out_norm[j], HG_VAL,
                           hgrn_w_out[j], ffn_norm[layer], ffn_w_in[layer], ffn_w_out[layer], final_norm, last)
    return x2.reshape(bsz, seq, d)
```
